```python
import jax, jax.numpy as jnp
from jax import lax
import numpy as np

D_MODEL = 2048
BATCH = 8
SEQ = 4096
DEPTH = 1
DEC_BATCH = 2
DEC_SEQ = 8192
PAST_LEN = 128

GRID_W = 64
Q_BLOCK = 128
N_HEADS = 16
N_KV_HEADS = 4
HEAD_DIM = 128
ROPE_THETA = 10000.0
SG_WIDTH = 2048
SG_GROUPS = 8
SG_CHUNK = 128
PEER_HEADS = 8
PEER_KEYS = 128
PEER_EXPERTS = PEER_KEYS * PEER_KEYS
PEER_TOPK = 16
PEER_DKEY = 256
TOKEN_BLOCK = 128
PLE_DIM = 256
EPS = 1e-6

Q_W = N_HEADS * HEAD_DIM
KV_W = N_KV_HEADS * HEAD_DIM
IN_W = Q_W + 2 * KV_W + 2 * SG_WIDTH + 2 * D_MODEL
SPLITS = [Q_W, Q_W + KV_W, Q_W + 2 * KV_W, Q_W + 2 * KV_W + SG_WIDTH,
          Q_W + 2 * KV_W + 2 * SG_WIDTH, Q_W + 2 * KV_W + 2 * SG_WIDTH + D_MODEL]

kernel_name = "hybrid_gated_attn_sgu_peer_encoder"


def rmsnorm(x, g):
    xf = x.astype(jnp.float32)
    y = xf * lax.rsqrt(jnp.mean(xf * xf, axis=-1, keepdims=True) + EPS)
    return (y * g.astype(jnp.float32)).astype(x.dtype)


def axial_rope(seq):
    rows = seq // GRID_W
    row = jnp.repeat(jnp.arange(rows, dtype=jnp.float32), GRID_W)
    col = jnp.tile(jnp.arange(GRID_W, dtype=jnp.float32), rows)
    n_axis_pairs = HEAD_DIM // 4
    freqs = ROPE_THETA ** (-jnp.arange(n_axis_pairs, dtype=jnp.float32) / n_axis_pairs)
    ang = jnp.concatenate([row[:, None] * freqs, col[:, None] * freqs], axis=-1)
    return jnp.cos(ang), jnp.sin(ang)


def apply_rope(x, cos, sin):
    b, s, h, d = x.shape
    xr = x.astype(jnp.float32).reshape(b, s, h, d // 2, 2)
    x0, x1 = xr[..., 0], xr[..., 1]
    c, sn = cos[None, :, None, :], sin[None, :, None, :]
    out = jnp.stack([x0 * c - x1 * sn, x0 * sn + x1 * c], axis=-1)
    return out.reshape(b, s, h, d).astype(x.dtype)


def block_attention(q, k, v):
    b, s, _, _ = q.shape
    nq = s // Q_BLOCK
    grp = N_HEADS // N_KV_HEADS
    scale = HEAD_DIM ** -0.5
    qb = q.reshape(b, nq, Q_BLOCK, N_KV_HEADS, grp, HEAD_DIM).transpose(1, 0, 2, 3, 4, 5)

    def one_block(qi):
        sc = jnp.einsum('bqkgd,bskd->bkgqs', qi, k).astype(jnp.float32) * scale
        pr = jax.nn.softmax(sc, axis=-1).astype(v.dtype)
        return jnp.einsum('bkgqs,bskd->bqkgd', pr, v)

    o = lax.map(one_block, qb)
    return o.transpose(1, 0, 2, 3, 4, 5).reshape(b, s, Q_W)


def spatial_gating(su, sv, g_sg, w_s, b_s):
    b, s, _ = su.shape
    u = jax.nn.gelu(su)
    v = rmsnorm(jax.nn.gelu(sv), g_sg)
    vc = v.reshape(b, s // SG_CHUNK, SG_CHUNK, SG_GROUPS, SG_WIDTH // SG_GROUPS)
    mixed = jnp.einsum('gts,bnsgc->bntgc', w_s, vc) + b_s.T[None, None, :, :, None]
    return u * mixed.reshape(b, s, SG_WIDTH)


def peer(a, w_pq, sub_keys, u_tab, v_tab):
    b, s, d = a.shape
    t = b * s
    xt = a.reshape(t, d)
    q = (xt @ w_pq).reshape(t, PEER_HEADS, 2, PEER_DKEY // 2)
    sc = jnp.einsum('thcd,hcnd->thcn', q, sub_keys).astype(jnp.float32)
    s_top, i_top = lax.top_k(sc, PEER_TOPK)
    cand = (s_top[:, :, 0, :, None] + s_top[:, :, 1, None, :]).reshape(t, PEER_HEADS, PEER_TOPK * PEER_TOPK)
    cidx = (i_top[:, :, 0, :, None] * PEER_KEYS + i_top[:, :, 1, None, :]).reshape(t, PEER_HEADS, PEER_TOPK * PEER_TOPK)
    best, pos = lax.top_k(cand, PEER_TOPK)
    eidx = jnp.take_along_axis(cidx, pos, axis=-1)
    gate = jax.nn.softmax(best, axis=-1).astype(a.dtype)
    nb = t // TOKEN_BLOCK
    kk = PEER_HEADS * PEER_TOPK
    xb = xt.reshape(nb, TOKEN_BLOCK, d)
    ib = eidx.reshape(nb, TOKEN_BLOCK, kk)
    gb = gate.reshape(nb, TOKEN_BLOCK, kk)

    def one_block(args):
        xi, ii, gi = args
        ue = u_tab[ii]
        hid = jax.nn.gelu(jnp.einsum('td,tkd->tk', xi, ue))
        ve = v_tab[ii]
        return jnp.einsum('tk,tkd->td', gi * hid, ve)

    out = lax.map(one_block, (xb, ib, gb))
    return out.reshape(b, s, d)


def encode(x, p, g_mix, w_in, g_q, g_k, g_sg, w_s, b_s, w_br_attn, w_br_sg, w_out,
           g_ffn, w_pq, sub_keys, u_tab, v_tab, g_ple, w_pg, b_pg, w_pe, g_final):
    b, s, _ = x.shape
    cos, sin = axial_rope(s)
    h = x
    for l in range(DEPTH):
        a = rmsnorm(h, g_mix[l])
        z = a @ w_in[l]
        q, k, v, su, sv, ga, gs = jnp.split(z, SPLITS, axis=-1)
        q = apply_rope(rmsnorm(q.reshape(b, s, N_HEADS, HEAD_DIM), g_q[l]), cos, sin)
        k = apply_rope(rmsnorm(k.reshape(b, s, N_KV_HEADS, HEAD_DIM), g_k[l]), cos, sin)
        v = v.reshape(b, s, N_KV_HEADS, HEAD_DIM)
        attn = block_attention(q, k, v) @ w_br_attn[l]
        sg = spatial_gating(su, sv, g_sg[l], w_s[l], b_s[l]) @ w_br_sg[l]
        merged = jax.nn.sigmoid(ga) * attn + jax.nn.sigmoid(gs) * sg
        h = h + merged @ w_out[l]
        h = h + peer(rmsnorm(h, g_ffn[l]), w_pq[l], sub_keys[l], u_tab[l], v_tab[l])
        ple_gate = jax.nn.sigmoid(rmsnorm(h, g_ple[l]) @ w_pg[l] + b_pg[l])
        h = h + ple_gate * (p[l] @ w_pe[l])
    return rmsnorm(h, g_final)


def setup_inputs(seed: int = 0) -> dict:
    key = jax.random.key(seed)
    ks = jax.random.split(key, 32)
    f32 = jnp.float32

    def nrm(k, shape, scale):
        return jax.random.normal(k, shape, f32) * scale

    def gain(k, shape):
        return 1.0 + 0.02 * jax.random.normal(k, shape, f32)

    L = DEPTH
    return {
        "x_prompt": nrm(ks[0], (BATCH, SEQ, D_MODEL), 1.0),
        "x_sample": nrm(ks[1], (DEC_BATCH, DEC_SEQ, D_MODEL), 1.0),
        "p_prompt": nrm(ks[2], (DEPTH, BATCH, SEQ, PLE_DIM), 1.0),
        "p_sample": nrm(ks[3], (DEPTH, DEC_BATCH, DEC_SEQ, PLE_DIM), 1.0),
        "g_mix": gain(ks[4], (L, D_MODEL)),
        "w_in": nrm(ks[5], (L, D_MODEL, IN_W), D_MODEL ** -0.5),
        "g_q": gain(ks[6], (L, HEAD_DIM)),
        "g_k": gain(ks[7], (L, HEAD_DIM)),
        "g_sg": gain(ks[8], (L, SG_WIDTH)),
        "w_s": nrm(ks[9], (L, SG_GROUPS, SG_CHUNK, SG_CHUNK), SG_CHUNK ** -0.5),
        "b_s": 1.0 + nrm(ks[10], (L, SG_GROUPS, SG_CHUNK), 0.1),
        "w_br_attn": nrm(ks[11], (L, Q_W, D_MODEL), Q_W ** -0.5),
        "w_br_sg": nrm(ks[12], (L, SG_WIDTH, D_MODEL), SG_WIDTH ** -0.5),
        "w_out": nrm(ks[13], (L, D_MODEL, D_MODEL), D_MODEL ** -0.5),
        "g_ffn": gain(ks[14], (L, D_MODEL)),
        "w_pq": nrm(ks[15], (L, D_MODEL, PEER_HEADS * PEER_DKEY), D_MODEL ** -0.5),
        "sub_keys": nrm(ks[16], (L, PEER_HEADS, 2, PEER_KEYS, PEER_DKEY // 2), (PEER_DKEY // 2) ** -0.5),
        "u_tab": nrm(ks[17], (L, PEER_EXPERTS, D_MODEL), D_MODEL ** -0.5),
        "v_tab": nrm(ks[18], (L, PEER_EXPERTS, D_MODEL), PEER_HEADS ** -0.5),
        "g_ple": gain(ks[19], (L, D_MODEL)),
        "w_pg": nrm(ks[20], (L, D_MODEL, D_MODEL), D_MODEL ** -0.5),
        "b_pg": nrm(ks[21], (L, D_MODEL), 0.02),
        "w_pe": nrm(ks[22], (L, PLE_DIM, D_MODEL), PLE_DIM ** -0.5),
        "g_final": gain(ks[23], (D_MODEL,)),
    }


def reference(x_prompt, x_sample, p_prompt, p_sample, g_mix, w_in, g_q, g_k, g_sg, w_s, b_s,
              w_br_attn, w_br_sg, w_out, g_ffn, w_pq, sub_keys, u_tab, v_tab, g_ple, w_pg,
              b_pg, w_pe, g_final):
    y_prompt = encode(x_prompt, p_prompt, g_mix, w_in, g_q, g_k, g_sg, w_s, b_s, w_br_attn,
                      w_br_sg, w_out, g_ffn, w_pq, sub_keys, u_tab, v_tab, g_ple, w_pg, b_pg,
                      w_pe, g_final)
    y_sample = encode(x_sample, p_sample, g_mix, w_in, g_q, g_k, g_sg, w_s, b_s, w_br_attn,
                      w_br_sg, w_out, g_ffn, w_pq, sub_keys, u_tab, v_tab, g_ple, w_pg, b_pg,
                      w_pe, g_final)
    return (y_prompt, y_sample)
```

```python
import functools
import math

import jax
import jax.numpy as jnp
import numpy as np
from jax import lax
from jax.experimental import pallas as pl
from jax.experimental.pallas import tpu as pltpu

F32 = jnp.float32
BF16 = jnp.bfloat16

D_MODEL = 2048
GRID_W = 64
N_HEADS = 16
N_KV_HEADS = 4
HEAD_DIM = 128
GROUP = N_HEADS // N_KV_HEADS
ROPE_THETA = 10000.0
SG_WIDTH = 2048
SG_GROUPS = 8
SG_CHUNK = 128
PEER_HEADS = 8
PEER_KEYS = 128
PEER_TOPK = 16
PLE_DIM = 256
EPS = 1e-6
Q_W = N_HEADS * HEAD_DIM
KV_W = N_KV_HEADS * HEAD_DIM
IN_W = Q_W + 2 * KV_W + 2 * SG_WIDTH + 2 * D_MODEL

COL_Q, COL_SU, COL_SV, COL_GA, COL_GS = 0, 1, 2, 3, 4
COL_K = 5 * D_MODEL // KV_W
COL_V = COL_K + 1

V7X_VMEM_LIMIT = 56 * 1024 * 1024
NEG_INF = float("-inf")


def _params(sem, vmem=V7X_VMEM_LIMIT):
    return pltpu.CompilerParams(dimension_semantics=sem, vmem_limit_bytes=vmem)


def _resident(shape):
    nd = len(shape)
    return pl.BlockSpec(shape, lambda *_: (0,) * nd, pipeline_mode=pl.Buffered(1))


def _gelu(x):
    return 0.5 * x * (1.0 + jnp.tanh(0.7978845608028654 * (x + 0.044715 * (x * x * x))))


def _sigmoid(x):
    return 1.0 / (1.0 + jnp.exp(-x))


def _rms(x, g):
    ms = jnp.mean(x * x, axis=-1, keepdims=True)
    return x * lax.rsqrt(ms + EPS) * g


def _in_proj_kernel(x_ref, g_ref, w_ref, o_ref, a_ref):
    @pl.when(pl.program_id(1) == 0)
    def _():
        a_ref[...] = _rms(x_ref[...], g_ref[...]).astype(BF16)

    o_ref[...] = jnp.dot(a_ref[...], w_ref[...], preferred_element_type=F32).astype(o_ref.dtype)


def in_proj(x, g, w, tm=512, tn=1024):
    t, d = x.shape
    n = w.shape[1]
    tm = min(tm, t)
    return pl.pallas_call(
        _in_proj_kernel,
        grid=(t // tm, n // tn),
        in_specs=[
            pl.BlockSpec((tm, d), lambda i, j: (i, 0)),
            pl.BlockSpec((1, d), lambda i, j: (0, 0)),
            pl.BlockSpec((d, tn), lambda i, j: (0, j)),
        ],
        out_specs=pl.BlockSpec((tm, tn), lambda i, j: (i, j)),
        out_shape=jax.ShapeDtypeStruct((t, n), BF16),
        scratch_shapes=[pltpu.VMEM((tm, d), BF16)],
        compiler_params=_params(("parallel", "arbitrary")),
        name="in_proj",
    )(x, g, w)


def _qkv_prep_kernel(zq_ref, zk_ref, zv_ref, gq_ref, gk_ref, cos_ref, sin_ref,
                     q_ref, k_ref, v_ref):
    cos = cos_ref[...]
    sin = sin_ref[...]

    def norm_rope(x, g, scale):
        y = _rms(x.astype(F32), g)
        return (y * cos + pltpu.roll(y, HEAD_DIM // 2, 1) * sin) * scale

    for h in range(N_HEADS):
        x = zq_ref[:, h * HEAD_DIM:(h + 1) * HEAD_DIM]
        q_ref[0, h] = norm_rope(x, gq_ref[...], HEAD_DIM ** -0.5).astype(BF16)
    for h in range(N_KV_HEADS):
        x = zk_ref[:, h * HEAD_DIM:(h + 1) * HEAD_DIM]
        k_ref[0, h] = norm_rope(x, gk_ref[...], 1.0).astype(BF16)
        v_ref[0, h] = zv_ref[:, h * HEAD_DIM:(h + 1) * HEAD_DIM]


def qkv_prep(z, gq, gk, cos, sin, b, s, ts=512):
    ts = min(ts, s)
    ns = s // ts
    head_spec = lambda nh: pl.BlockSpec((1, nh, ts, HEAD_DIM), lambda bi, si: (bi, 0, si, 0))
    return pl.pallas_call(
        _qkv_prep_kernel,
        grid=(b, ns),
        in_specs=[
            pl.BlockSpec((ts, Q_W), lambda bi, si: (bi * ns + si, COL_Q)),
            pl.BlockSpec((ts, KV_W), lambda bi, si: (bi * ns + si, COL_K)),
            pl.BlockSpec((ts, KV_W), lambda bi, si: (bi * ns + si, COL_V)),
            pl.BlockSpec((1, HEAD_DIM), lambda bi, si: (0, 0)),
            pl.BlockSpec((1, HEAD_DIM), lambda bi, si: (0, 0)),
            pl.BlockSpec((ts, HEAD_DIM), lambda bi, si: (si, 0)),
            pl.BlockSpec((ts, HEAD_DIM), lambda bi, si: (si, 0)),
        ],
        out_specs=[head_spec(N_HEADS), head_spec(N_KV_HEADS), head_spec(N_KV_HEADS)],
        out_shape=[
            jax.ShapeDtypeStruct((b, N_HEADS, s, HEAD_DIM), BF16),
            jax.ShapeDtypeStruct((b, N_KV_HEADS, s, HEAD_DIM), BF16),
            jax.ShapeDtypeStruct((b, N_KV_HEADS, s, HEAD_DIM), BF16),
        ],
        compiler_params=_params(("parallel", "parallel")),
        name="qkv_prep",
    )(z, z, z, gq, gk, cos, sin)


def _attn_kernel(q_ref, k_ref, v_ref, o_ref, m_ref, l_ref, acc_ref, *, tk):
    tq = q_ref.shape[2]
    rows = GROUP * tq
    q = q_ref[0].reshape(rows, HEAD_DIM)
    m_ref[...] = jnp.full(m_ref.shape, NEG_INF, F32)
    l_ref[...] = jnp.zeros(l_ref.shape, F32)
    acc_ref[...] = jnp.zeros(acc_ref.shape, F32)

    def body(i, carry):
        start = pl.multiple_of(i * tk, tk)
        k = k_ref[0, 0, pl.ds(start, tk), :]
        v = v_ref[0, 0, pl.ds(start, tk), :]
        s = lax.dot_general(q, k, (((1,), (1,)), ((), ())), preferred_element_type=F32)
        m_prev = m_ref[...]
        m_new = jnp.maximum(m_prev, jnp.max(s, axis=-1, keepdims=True))
        alpha = jnp.exp(m_prev - m_new)
        p = jnp.exp(s - m_new)
        l_ref[...] = alpha * l_ref[...] + jnp.sum(p, axis=-1, keepdims=True)
        acc_ref[...] = alpha * acc_ref[...] + jnp.dot(p.astype(BF16), v, preferred_element_type=F32)
        m_ref[...] = m_new
        return carry

    lax.fori_loop(0, k_ref.shape[2] // tk, body, 0)
    o_ref[0] = (acc_ref[...] / l_ref[...]).reshape(GROUP, tq, HEAD_DIM).astype(o_ref.dtype)


def attention(q, k, v, tq=256, tk=512):
    b, _, s, _ = q.shape
    tq = min(tq, s)
    tk = min(tk, s)
    rows = GROUP * tq
    return pl.pallas_call(
        functools.partial(_attn_kernel, tk=tk),
        grid=(b, N_KV_HEADS, s // tq),
        in_specs=[
            pl.BlockSpec((1, GROUP, tq, HEAD_DIM), lambda bi, hi, qi: (bi, hi, qi, 0)),
            pl.BlockSpec((1, 1, s, HEAD_DIM), lambda bi, hi, qi: (bi, hi, 0, 0)),
            pl.BlockSpec((1, 1, s, HEAD_DIM), lambda bi, hi, qi: (bi, hi, 0, 0)),
        ],
        out_specs=pl.BlockSpec((1, GROUP, tq, HEAD_DIM), lambda bi, hi, qi: (bi, hi, qi, 0)),
        out_shape=jax.ShapeDtypeStruct(q.shape, BF16),
        scratch_shapes=[
            pltpu.VMEM((rows, 1), F32),
            pltpu.VMEM((rows, 1), F32),
            pltpu.VMEM((rows, HEAD_DIM), F32),
        ],
        compiler_params=_params(("parallel", "parallel", "arbitrary")),
        name="attention",
    )(q, k, v)


def _sgu_kernel(su_ref, sv_ref, g_ref, ws_ref, bias_ref, o_ref):
    gw = SG_WIDTH // SG_GROUPS
    for c in range(su_ref.shape[0] // SG_CHUNK):
        r = slice(c * SG_CHUNK, (c + 1) * SG_CHUNK)
        v = _rms(_gelu(sv_ref[r, :].astype(F32)), g_ref[...]).astype(BF16)
        for g in range(SG_GROUPS):
            cs = slice(g * gw, (g + 1) * gw)
            mixed = jnp.dot(ws_ref[g], v[:, cs], preferred_element_type=F32) + bias_ref[:, cs]
            o_ref[r, cs] = (_gelu(su_ref[r, cs].astype(F32)) * mixed).astype(o_ref.dtype)


def spatial_gating(z, g_sg, w_s, bias_full, tm=512):
    t = z.shape[0]
    tm = min(tm, t)
    return pl.pallas_call(
        _sgu_kernel,
        grid=(t // tm,),
        in_specs=[
            pl.BlockSpec((tm, SG_WIDTH), lambda i: (i, COL_SU)),
            pl.BlockSpec((tm, SG_WIDTH), lambda i: (i, COL_SV)),
            pl.BlockSpec((1, SG_WIDTH), lambda i: (0, 0)),
            pl.BlockSpec((SG_GROUPS, SG_CHUNK, SG_CHUNK), lambda i: (0, 0, 0)),
            pl.BlockSpec((SG_CHUNK, SG_WIDTH), lambda i: (0, 0)),
        ],
        out_specs=pl.BlockSpec((tm, SG_WIDTH), lambda i: (i, 0)),
        out_shape=jax.ShapeDtypeStruct((t, SG_WIDTH), BF16),
        compiler_params=_params(("parallel",)),
        name="spatial_gating",
    )(z, z, g_sg, w_s, bias_full)


def _merge_kernel(o_ref, sg_ref, ga_ref, gs_ref, wa_ref, ws_ref, out_ref):
    attn = jnp.concatenate([o_ref[0, h] for h in range(N_HEADS)], axis=-1)
    ya = jnp.dot(attn, wa_ref[...], preferred_element_type=F32)
    ys = jnp.dot(sg_ref[...], ws_ref[...], preferred_element_type=F32)
    merged = _sigmoid(ga_ref[...].astype(F32)) * ya + _sigmoid(gs_ref[...].astype(F32)) * ys
    out_ref[...] = merged.astype(out_ref.dtype)


def branch_merge(o, sg, z, wa, ws, tm=256):
    b, _, s, _ = o.shape
    tm = min(tm, s)
    ns = s // tm
    row = lambda bi, si: bi * ns + si
    return pl.pallas_call(
        _merge_kernel,
        grid=(b, ns),
        in_specs=[
            pl.BlockSpec((1, N_HEADS, tm, HEAD_DIM), lambda bi, si: (bi, 0, si, 0)),
            pl.BlockSpec((tm, D_MODEL), lambda bi, si: (row(bi, si), 0)),
            pl.BlockSpec((tm, D_MODEL), lambda bi, si: (row(bi, si), COL_GA)),
            pl.BlockSpec((tm, D_MODEL), lambda bi, si: (row(bi, si), COL_GS)),
            _resident(wa.shape),
            _resident(ws.shape),
        ],
        out_specs=pl.BlockSpec((tm, D_MODEL), lambda bi, si: (row(bi, si), 0)),
        out_shape=jax.ShapeDtypeStruct((b * s, D_MODEL), BF16),
        compiler_params=_params(("parallel", "parallel")),
        name="branch_merge",
    )(o, sg, z, z, wa, ws)


def _out_proj_kernel(m_ref, x_ref, w_ref, g_ref, h_ref, a_ref, at_ref):
    h = x_ref[...] + jnp.dot(m_ref[...], w_ref[...], preferred_element_type=F32)
    h_ref[...] = h
    a = _rms(h, g_ref[...])
    a_ref[...] = a.astype(BF16)
    at_ref[...] = a.T.astype(BF16)


def out_proj(merged, x, w, g, tm=256):
    t, d = x.shape
    tm = min(tm, t)
    return pl.pallas_call(
        _out_proj_kernel,
        grid=(t // tm,),
        in_specs=[
            pl.BlockSpec((tm, d), lambda i: (i, 0)),
            pl.BlockSpec((tm, d), lambda i: (i, 0)),
            _resident(w.shape),
            pl.BlockSpec((1, d), lambda i: (0, 0)),
        ],
        out_specs=[
            pl.BlockSpec((tm, d), lambda i: (i, 0)),
            pl.BlockSpec((tm, d), lambda i: (i, 0)),
            pl.BlockSpec((d, tm), lambda i: (0, i)),
        ],
        out_shape=[
            jax.ShapeDtypeStruct((t, d), F32),
            jax.ShapeDtypeStruct((t, d), BF16),
            jax.ShapeDtypeStruct((d, t), BF16),
        ],
        compiler_params=_params(("parallel",)),
        name="out_proj",
    )(merged, x, w, g)


def _top16(s):
    n = s.shape[0]
    iota = lax.broadcasted_iota(jnp.int32, s.shape, 0)
    rank = jnp.full(s.shape, PEER_TOPK, jnp.int32)
    vals = []
    for k in range(PEER_TOPK):
        m = jnp.max(s, axis=0, keepdims=True)
        idx = jnp.min(jnp.where(s == m, iota, n), axis=0, keepdims=True)
        sel = iota == idx
        rank = jnp.where(sel, k, rank)
        s = jnp.where(sel, NEG_INF, s)
        vals.append(m)
    return vals, rank


def _route_kernel(a_ref, w_ref, keys_ref, out_ref):
    q = jnp.dot(a_ref[...], w_ref[...], preferred_element_type=F32)
    tm = q.shape[0]
    for h in range(PEER_HEADS):
        planes = []
        for c in range(2):
            hc = 2 * h + c
            qs = q[:, hc * PEER_KEYS:(hc + 1) * PEER_KEYS].astype(BF16)
            s = lax.dot_general(keys_ref[hc], qs, (((1,), (1,)), ((), ())),
                                preferred_element_type=F32)
            vals, rank = _top16(s)
            planes.append((s, vals, rank))
        (s0, v0, rank0), (s1, v1, rank1) = planes
        v1s = jnp.concatenate(v1, axis=0)
        cand = jnp.concatenate([v0[a] + v1s for a in range(PEER_TOPK)], axis=0)
        _, crank = _top16(cand)
        chosen = crank < PEER_TOPK
        mx = v0[0] + v1[0]
        zsum = jnp.sum(jnp.where(chosen, jnp.exp(cand - mx), 0.0), axis=0, keepdims=True)
        cnt = jnp.zeros(s0.shape, F32)
        for a in range(PEER_TOPK):
            c_a = jnp.sum(jnp.where(chosen[a * PEER_TOPK:(a + 1) * PEER_TOPK], 1.0, 0.0),
                          axis=0, keepdims=True)
            cnt = jnp.where(rank0 == a, c_a, cnt)
        out_ref[h, 0] = jnp.where(rank0 < PEER_TOPK, jnp.exp(s0 - v0[0]), 0.0) / zsum
        out_ref[h, 1] = cnt
        out_ref[h, 2] = jnp.where(rank1 < PEER_TOPK, jnp.exp(s1 - v1[0]), 0.0)
        out_ref[h, 3] = rank1.astype(F32)


def peer_route(a2, w_pq, keys, tm=256):
    t, d = a2.shape
    tm = min(tm, t)
    return pl.pallas_call(
        _route_kernel,
        grid=(t // tm,),
        in_specs=[
            pl.BlockSpec((tm, d), lambda i: (i, 0)),
            _resident(w_pq.shape),
            _resident(keys.shape),
        ],
        out_specs=pl.BlockSpec((PEER_HEADS, 4, PEER_KEYS, tm), lambda i: (0, 0, 0, i)),
        out_shape=jax.ShapeDtypeStruct((PEER_HEADS, 4, PEER_KEYS, t), F32),
        compiler_params=_params(("parallel",)),
        name="peer_route",
    )(a2, w_pq, keys)


def _peer_kernel(at_ref, r_ref, u_ref, vt_ref, h_ref, o_ref, acc_ref, p_ref, *, ib):
    ci = pl.program_id(1)

    @pl.when(ci == 0)
    def _():
        acc_ref[...] = jnp.zeros(acc_ref.shape, F32)

    hid = _gelu(jnp.dot(u_ref[...], at_ref[...], preferred_element_type=F32))
    for ii in range(ib):
        row = ci * ib + ii
        gate = None
        for h in range(PEER_HEADS):
            a_i = r_ref[h, 0, pl.ds(row, 1), :]
            c_i = r_ref[h, 1, pl.ds(row, 1), :]
            w = jnp.where(r_ref[h, 3] < c_i, r_ref[h, 2] * a_i, 0.0)
            gate = w if gate is None else gate + w
        rs = slice(ii * PEER_KEYS, (ii + 1) * PEER_KEYS)
        p_ref[rs, :] = (gate * hid[rs, :]).astype(BF16)
    acc_ref[...] += jnp.dot(vt_ref[...], p_ref[...], preferred_element_type=F32)

    @pl.when(ci == pl.num_programs(1) - 1)
    def _():
        o_ref[...] = h_ref[...] + acc_ref[...].T


def peer_experts(a2t, route, u_bf, vt_bf, h1, tb=512, ib=8):
    d, t = a2t.shape
    tb = min(tb, t)
    ne = u_bf.shape[0]
    ec = ib * PEER_KEYS
    return pl.pallas_call(
        functools.partial(_peer_kernel, ib=ib),
        grid=(t // tb, ne // ec),
        in_specs=[
            pl.BlockSpec((d, tb), lambda i, c: (0, i), pipeline_mode=pl.Buffered(1)),
            pl.BlockSpec((PEER_HEADS, 4, PEER_KEYS, tb), lambda i, c: (0, 0, 0, i),
                         pipeline_mode=pl.Buffered(1)),
            pl.BlockSpec((ec, d), lambda i, c: (c, 0)),
            pl.BlockSpec((d, ec), lambda i, c: (0, c)),
            pl.BlockSpec((tb, d), lambda i, c: (i, 0), pipeline_mode=pl.Buffered(1)),
        ],
        out_specs=pl.BlockSpec((tb, d), lambda i, c: (i, 0)),
        out_shape=jax.ShapeDtypeStruct((t, d), F32),
        scratch_shapes=[pltpu.VMEM((d, tb), F32), pltpu.VMEM((ec, tb), BF16)],
        compiler_params=_params(("parallel", "arbitrary")),
        name="peer_experts",
    )(a2t, route, u_bf, vt_bf, h1)


def _ple_kernel(h_ref, p_ref, gp_ref, wg_ref, bg_ref, we_ref, gf_ref, y_ref):
    h = h_ref[...]
    a = _rms(h, gp_ref[...]).astype(BF16)
    gate = _sigmoid(jnp.dot(a, wg_ref[...], preferred_element_type=F32) + bg_ref[...])
    pe = jnp.dot(p_ref[...].astype(BF16), we_ref[...], preferred_element_type=F32)
    y_ref[...] = _rms(h + gate * pe, gf_ref[...])


def ple_final(h2, p, g_ple, w_pg, b_pg, w_pe, g_final, tm=256):
    t, d = h2.shape
    tm = min(tm, t)
    vec = pl.BlockSpec((1, d), lambda i: (0, 0))
    return pl.pallas_call(
        _ple_kernel,
        grid=(t // tm,),
        in_specs=[
            pl.BlockSpec((tm, d), lambda i: (i, 0)),
            pl.BlockSpec((tm, PLE_DIM), lambda i: (i, 0)),
            vec,
            _resident(w_pg.shape),
            vec,
            _resident(w_pe.shape),
            vec,
        ],
        out_specs=pl.BlockSpec((tm, d), lambda i: (i, 0)),
        out_shape=jax.ShapeDtypeStruct((t, d), F32),
        compiler_params=_params(("parallel",)),
        name="ple_final",
    )(h2, p, g_ple, w_pg, b_pg, w_pe, g_final)


def _rope_tables(seq):
    rows = seq // GRID_W
    row = jnp.repeat(jnp.arange(rows, dtype=F32), GRID_W)
    col = jnp.tile(jnp.arange(GRID_W, dtype=F32), rows)
    n_axis_pairs = HEAD_DIM // 4
    freqs = ROPE_THETA ** (-jnp.arange(n_axis_pairs, dtype=F32) / n_axis_pairs)
    ang = jnp.concatenate([row[:, None] * freqs, col[:, None] * freqs], axis=-1)
    cos, sin = jnp.cos(ang), jnp.sin(ang)
    return jnp.concatenate([cos, cos], axis=-1), jnp.concatenate([-sin, sin], axis=-1)


def _head_perm():
    return np.concatenate([np.arange(0, HEAD_DIM, 2), np.arange(1, HEAD_DIM, 2)])


def _in_proj_columns():
    hp = _head_perm()
    q_cols = (np.arange(N_HEADS)[:, None] * HEAD_DIM + hp[None, :]).reshape(-1)
    k_cols = Q_W + (np.arange(N_KV_HEADS)[:, None] * HEAD_DIM + hp[None, :]).reshape(-1)
    v_cols = Q_W + KV_W + np.arange(KV_W)
    rest = Q_W + 2 * KV_W + np.arange(2 * SG_WIDTH + 2 * D_MODEL)
    return np.concatenate([q_cols, rest, k_cols, v_cols])


def _encode(x, p, wts):
    b, s, d = x.shape
    t = b * s
    x2 = x.reshape(t, d)
    cos, sin = _rope_tables(s)
    z = in_proj(x2, wts["g_mix"], wts["w_in"])
    q, k, v = qkv_prep(z, wts["g_q"], wts["g_k"], cos, sin, b, s)
    o = attention(q, k, v)
    sg = spatial_gating(z, wts["g_sg"], wts["w_s"], wts["sg_bias"])
    merged = branch_merge(o, sg, z, wts["w_br_attn"], wts["w_br_sg"])
    h1, a2, a2t = out_proj(merged, x2, wts["w_out"], wts["g_ffn"])
    route = peer_route(a2, wts["w_pq"], wts["keys"])
    h2 = peer_experts(a2t, route, wts["u"], wts["vt"], h1)
    y = ple_final(h2, p.reshape(t, PLE_DIM), wts["g_ple"], wts["w_pg"], wts["b_pg"],
                  wts["w_pe"], wts["g_final"])
    return y.reshape(b, s, d)


def _prepare_weights(g_mix, w_in, g_q, g_k, g_sg, w_s, b_s, w_br_attn, w_br_sg, w_out, g_ffn,
                     w_pq, sub_keys, u_tab, v_tab, g_ple, w_pg, b_pg, w_pe, g_final):
    assert w_in.shape[0] == 1, "single-layer encoder"
    hp = _head_perm()
    gw = SG_WIDTH // SG_GROUPS
    return {
        "g_mix": g_mix[0][None, :],
        "w_in": w_in[0][:, _in_proj_columns()].astype(BF16),
        "g_q": g_q[0][hp][None, :],
        "g_k": g_k[0][hp][None, :],
        "g_sg": g_sg[0][None, :],
        "w_s": w_s[0].astype(BF16),
        "sg_bias": jnp.repeat(b_s[0].T, gw, axis=1),
        "w_br_attn": w_br_attn[0].astype(BF16),
        "w_br_sg": w_br_sg[0].astype(BF16),
        "w_out": w_out[0].astype(BF16),
        "g_ffn": g_ffn[0][None, :],
        "w_pq": w_pq[0].astype(BF16),
        "keys": sub_keys[0].reshape(2 * PEER_HEADS, PEER_KEYS, -1).astype(BF16),
        "u": u_tab[0].astype(BF16),
        "vt": v_tab[0].T.astype(BF16),
        "g_ple": g_ple[0][None, :],
        "w_pg": w_pg[0].astype(BF16),
        "b_pg": b_pg[0][None, :],
        "w_pe": w_pe[0].astype(BF16),
        "g_final": g_final[None, :],
    }


def kernel(x_prompt, x_sample, p_prompt, p_sample, g_mix, w_in, g_q, g_k, g_sg, w_s, b_s,
           w_br_attn, w_br_sg, w_out, g_ffn, w_pq, sub_keys, u_tab, v_tab, g_ple, w_pg,
           b_pg, w_pe, g_final):
    wts = _prepare_weights(g_mix, w_in, g_q, g_k, g_sg, w_s, b_s, w_br_attn, w_br_sg, w_out,
                           g_ffn, w_pq, sub_keys, u_tab, v_tab, g_ple, w_pg, b_pg, w_pe, g_final)
    return (_encode(x_prompt, p_prompt[0], wts), _encode(x_sample, p_sample[0], wts))
```

```python
import functools
import math

import jax
import jax.numpy as jnp
import numpy as np
from jax import lax
from jax.experimental import pallas as pl
from jax.experimental.pallas import tpu as pltpu

F32 = jnp.float32
BF16 = jnp.bfloat16

D_MODEL = 2048
GRID_W = 64
N_HEADS = 16
N_KV_HEADS = 4
HEAD_DIM = 128
GROUP = N_HEADS // N_KV_HEADS
ROPE_THETA = 10000.0
SG_WIDTH = 2048
SG_GROUPS = 8
SG_CHUNK = 128
PEER_HEADS = 8
PEER_KEYS = 128
PEER_TOPK = 16
PLE_DIM = 256
EPS = 1e-6
Q_W = N_HEADS * HEAD_DIM
KV_W = N_KV_HEADS * HEAD_DIM
IN_W = Q_W + 2 * KV_W + 2 * SG_WIDTH + 2 * D_MODEL

COL_Q, COL_SU, COL_SV, COL_GA, COL_GS = 0, 1, 2, 3, 4
COL_K = 5 * D_MODEL // KV_W
COL_V = COL_K + 1

V7X_VMEM_LIMIT = 56 * 1024 * 1024
NEG_INF = float("-inf")
LOG2E = 1.4426950408889634


def _params(sem, vmem=V7X_VMEM_LIMIT):
    return pltpu.CompilerParams(dimension_semantics=sem, vmem_limit_bytes=vmem)


def _resident(shape):
    nd = len(shape)
    return pl.BlockSpec(shape, lambda *_: (0,) * nd, pipeline_mode=pl.Buffered(1))


def _gelu(x):
    return 0.5 * x * (1.0 + jnp.tanh(0.7978845608028654 * (x + 0.044715 * (x * x * x))))


def _sigmoid(x):
    return 1.0 / (1.0 + jnp.exp(-x))


def _rms(x, g):
    ms = jnp.mean(x * x, axis=-1, keepdims=True)
    return x * lax.rsqrt(ms + EPS) * g


def _in_proj_kernel(x_ref, g_ref, w_ref, o_ref, a_ref):
    @pl.when(pl.program_id(1) == 0)
    def _():
        a_ref[...] = _rms(x_ref[...], g_ref[...]).astype(BF16)

    o_ref[...] = jnp.dot(a_ref[...], w_ref[...], preferred_element_type=F32).astype(o_ref.dtype)


def in_proj(x, g, w, tm=512, tn=1024):
    t, d = x.shape
    n = w.shape[1]
    tm = min(tm, t)
    return pl.pallas_call(
        _in_proj_kernel,
        grid=(t // tm, n // tn),
        in_specs=[
            pl.BlockSpec((tm, d), lambda i, j: (i, 0)),
            pl.BlockSpec((1, d), lambda i, j: (0, 0)),
            pl.BlockSpec((d, tn), lambda i, j: (0, j)),
        ],
        out_specs=pl.BlockSpec((tm, tn), lambda i, j: (i, j)),
        out_shape=jax.ShapeDtypeStruct((t, n), BF16),
        scratch_shapes=[pltpu.VMEM((tm, d), BF16)],
        compiler_params=_params(("parallel", "arbitrary")),
        name="in_proj",
    )(x, g, w)


def _qkv_prep_kernel(zq_ref, zk_ref, zv_ref, gq_ref, gk_ref, cos_ref, sin_ref,
                     q_ref, k_ref, v_ref):
    cos = cos_ref[...]
    sin = sin_ref[...]

    def norm_rope(x, g, scale):
        y = _rms(x.astype(F32), g)
        return (y * cos + pltpu.roll(y, HEAD_DIM // 2, 1) * sin) * scale

    ones = jnp.ones((zv_ref.shape[0], HEAD_DIM), BF16)
    for h in range(N_HEADS):
        x = zq_ref[:, h * HEAD_DIM:(h + 1) * HEAD_DIM]
        q_ref[0, h] = norm_rope(x, gq_ref[...], LOG2E * HEAD_DIM ** -0.5).astype(BF16)
    for h in range(N_KV_HEADS):
        x = zk_ref[:, h * HEAD_DIM:(h + 1) * HEAD_DIM]
        k_ref[0, h] = norm_rope(x, gk_ref[...], 1.0).astype(BF16)
        v_ref[0, h, :, :HEAD_DIM] = zv_ref[:, h * HEAD_DIM:(h + 1) * HEAD_DIM]
        v_ref[0, h, :, HEAD_DIM:] = ones


def qkv_prep(z, gq, gk, cos, sin, b, s, ts=512):
    ts = min(ts, s)
    ns = s // ts
    head_spec = lambda nh, w=HEAD_DIM: pl.BlockSpec((1, nh, ts, w), lambda bi, si: (bi, 0, si, 0))
    return pl.pallas_call(
        _qkv_prep_kernel,
        grid=(b, ns),
        in_specs=[
            pl.BlockSpec((ts, Q_W), lambda bi, si: (bi * ns + si, COL_Q)),
            pl.BlockSpec((ts, KV_W), lambda bi, si: (bi * ns + si, COL_K)),
            pl.BlockSpec((ts, KV_W), lambda bi, si: (bi * ns + si, COL_V)),
            pl.BlockSpec((1, HEAD_DIM), lambda bi, si: (0, 0)),
            pl.BlockSpec((1, HEAD_DIM), lambda bi, si: (0, 0)),
            pl.BlockSpec((ts, HEAD_DIM), lambda bi, si: (si, 0)),
            pl.BlockSpec((ts, HEAD_DIM), lambda bi, si: (si, 0)),
        ],
        out_specs=[head_spec(N_HEADS), head_spec(N_KV_HEADS), head_spec(N_KV_HEADS, 2 * HEAD_DIM)],
        out_shape=[
            jax.ShapeDtypeStruct((b, N_HEADS, s, HEAD_DIM), BF16),
            jax.ShapeDtypeStruct((b, N_KV_HEADS, s, HEAD_DIM), BF16),
            jax.ShapeDtypeStruct((b, N_KV_HEADS, s, 2 * HEAD_DIM), BF16),
        ],
        compiler_params=_params(("parallel", "parallel")),
        name="qkv_prep",
    )(z, z, z, gq, gk, cos, sin)


def _attn_kernel(q_ref, k_ref, v_ref, o_ref, m_ref, acc_ref, *, tk):
    tq = q_ref.shape[2]
    m_ref[...] = jnp.full(m_ref.shape, NEG_INF, F32)
    acc_ref[...] = jnp.zeros(acc_ref.shape, F32)

    def body(i, carry):
        start = pl.multiple_of(i * tk, tk)
        k = k_ref[0, 0, pl.ds(start, tk), :]
        v = v_ref[0, 0, pl.ds(start, tk), :]
        for g in range(GROUP):
            s = lax.dot_general(q_ref[0, g], k, (((1,), (1,)), ((), ())),
                                preferred_element_type=F32)
            m_prev = m_ref[g]
            m_new = jnp.maximum(m_prev, jnp.max(s, axis=-1, keepdims=True))
            alpha = jnp.exp2(m_prev - m_new)
            p = jnp.concatenate(
                [jnp.exp2(s[:, c * HEAD_DIM:(c + 1) * HEAD_DIM] - m_new).astype(BF16)
                 for c in range(tk // HEAD_DIM)], axis=-1)
            pv = jnp.dot(p, v, preferred_element_type=F32)
            acc_ref[g] = acc_ref[g] * jnp.concatenate([alpha, alpha], axis=-1) + pv
            m_ref[g] = m_new
        return carry

    lax.fori_loop(0, k_ref.shape[2] // tk, body, 0, unroll=True)
    for g in range(GROUP):
        acc = acc_ref[g]
        o_ref[0, g] = (acc[:, :HEAD_DIM] / acc[:, HEAD_DIM:]).astype(o_ref.dtype)


def attention(q, k, v1, tq=256, tk=1024):
    b, _, s, _ = q.shape
    tq = min(tq, s)
    tk = min(tk, s)
    return pl.pallas_call(
        functools.partial(_attn_kernel, tk=tk),
        grid=(b, N_KV_HEADS, s // tq),
        in_specs=[
            pl.BlockSpec((1, GROUP, tq, HEAD_DIM), lambda bi, hi, qi: (bi, hi, qi, 0)),
            pl.BlockSpec((1, 1, s, HEAD_DIM), lambda bi, hi, qi: (bi, hi, 0, 0)),
            pl.BlockSpec((1, 1, s, 2 * HEAD_DIM), lambda bi, hi, qi: (bi, hi, 0, 0)),
        ],
        out_specs=pl.BlockSpec((1, GROUP, tq, HEAD_DIM), lambda bi, hi, qi: (bi, hi, qi, 0)),
        out_shape=jax.ShapeDtypeStruct(q.shape, BF16),
        scratch_shapes=[
            pltpu.VMEM((GROUP, tq, HEAD_DIM), F32),
            pltpu.VMEM((GROUP, tq, 2 * HEAD_DIM), F32),
        ],
        compiler_params=_params(("parallel", "parallel", "arbitrary")),
        name="attention",
    )(q, k, v1)


def _sgu_kernel(su_ref, sv_ref, g_ref, ws_ref, bias_ref, o_ref):
    gw = SG_WIDTH // SG_GROUPS
    for c in range(su_ref.shape[0] // SG_CHUNK):
        r = slice(c * SG_CHUNK, (c + 1) * SG_CHUNK)
        v = _rms(_gelu(sv_ref[r, :].astype(F32)), g_ref[...]).astype(BF16)
        for g in range(SG_GROUPS):
            cs = slice(g * gw, (g + 1) * gw)
            mixed = jnp.dot(ws_ref[g], v[:, cs], preferred_element_type=F32) + bias_ref[:, cs]
            o_ref[r, cs] = (_gelu(su_ref[r, cs].astype(F32)) * mixed).astype(o_ref.dtype)


def spatial_gating(z, g_sg, w_s, bias_full, tm=512):
    t = z.shape[0]
    tm = min(tm, t)
    return pl.pallas_call(
        _sgu_kernel,
        grid=(t // tm,),
        in_specs=[
            pl.BlockSpec((tm, SG_WIDTH), lambda i: (i, COL_SU)),
            pl.BlockSpec((tm, SG_WIDTH), lambda i: (i, COL_SV)),
            pl.BlockSpec((1, SG_WIDTH), lambda i: (0, 0)),
            pl.BlockSpec((SG_GROUPS, SG_CHUNK, SG_CHUNK), lambda i: (0, 0, 0)),
            pl.BlockSpec((SG_CHUNK, SG_WIDTH), lambda i: (0, 0)),
        ],
        out_specs=pl.BlockSpec((tm, SG_WIDTH), lambda i: (i, 0)),
        out_shape=jax.ShapeDtypeStruct((t, SG_WIDTH), BF16),
        compiler_params=_params(("parallel",)),
        name="spatial_gating",
    )(z, z, g_sg, w_s, bias_full)


def _merge_kernel(o_ref, sg_ref, ga_ref, gs_ref, wa_ref, ws_ref, out_ref):
    attn = jnp.concatenate([o_ref[0, h] for h in range(N_HEADS)], axis=-1)
    ya = jnp.dot(attn, wa_ref[...], preferred_element_type=F32)
    ys = jnp.dot(sg_ref[...], ws_ref[...], preferred_element_type=F32)
    merged = _sigmoid(ga_ref[...].astype(F32)) * ya + _sigmoid(gs_ref[...].astype(F32)) * ys
    out_ref[...] = merged.astype(out_ref.dtype)


def branch_merge(o, sg, z, wa, ws, tm=256):
    b, _, s, _ = o.shape
    tm = min(tm, s)
    ns = s // tm
    row = lambda bi, si: bi * ns + si
    return pl.pallas_call(
        _merge_kernel,
        grid=(b, ns),
        in_specs=[
            pl.BlockSpec((1, N_HEADS, tm, HEAD_DIM), lambda bi, si: (bi, 0, si, 0)),
            pl.BlockSpec((tm, D_MODEL), lambda bi, si: (row(bi, si), 0)),
            pl.BlockSpec((tm, D_MODEL), lambda bi, si: (row(bi, si), COL_GA)),
            pl.BlockSpec((tm, D_MODEL), lambda bi, si: (row(bi, si), COL_GS)),
            _resident(wa.shape),
            _resident(ws.shape),
        ],
        out_specs=pl.BlockSpec((tm, D_MODEL), lambda bi, si: (row(bi, si), 0)),
        out_shape=jax.ShapeDtypeStruct((b * s, D_MODEL), BF16),
        compiler_params=_params(("parallel", "parallel")),
        name="branch_merge",
    )(o, sg, z, z, wa, ws)


def _out_proj_kernel(m_ref, x_ref, w_ref, g_ref, h_ref, a_ref, at_ref):
    h = x_ref[...] + jnp.dot(m_ref[...], w_ref[...], preferred_element_type=F32)
    h_ref[...] = h
    a = _rms(h, g_ref[...])
    a_ref[...] = a.astype(BF16)
    at_ref[...] = a.T.astype(BF16)


def out_proj(merged, x, w, g, tm=256):
    t, d = x.shape
    tm = min(tm, t)
    return pl.pallas_call(
        _out_proj_kernel,
        grid=(t // tm,),
        in_specs=[
            pl.BlockSpec((tm, d), lambda i: (i, 0)),
            pl.BlockSpec((tm, d), lambda i: (i, 0)),
            _resident(w.shape),
            pl.BlockSpec((1, d), lambda i: (0, 0)),
        ],
        out_specs=[
            pl.BlockSpec((tm, d), lambda i: (i, 0)),
            pl.BlockSpec((tm, d), lambda i: (i, 0)),
            pl.BlockSpec((d, tm), lambda i: (0, i)),
        ],
        out_shape=[
            jax.ShapeDtypeStruct((t, d), F32),
            jax.ShapeDtypeStruct((t, d), BF16),
            jax.ShapeDtypeStruct((d, t), BF16),
        ],
        compiler_params=_params(("parallel",)),
        name="out_proj",
    )(merged, x, w, g)


def _top16(s):
    n = s.shape[0]
    iota = lax.broadcasted_iota(jnp.int32, s.shape, 0)
    rank = jnp.full(s.shape, PEER_TOPK, jnp.int32)
    vals = []
    for k in range(PEER_TOPK):
        m = jnp.max(s, axis=0, keepdims=True)
        idx = jnp.min(jnp.where(s == m, iota, n), axis=0, keepdims=True)
        sel = iota == idx
        rank = jnp.where(sel, k, rank)
        s = jnp.where(sel, NEG_INF, s)
        vals.append(m)
    return vals, rank


def _route_kernel(a_ref, w_ref, keys_ref, ac_ref, br_ref):
    q = jnp.dot(a_ref[...], w_ref[...], preferred_element_type=F32)
    tm = q.shape[0]
    for h in range(PEER_HEADS):
        planes = []
        for c in range(2):
            hc = 2 * h + c
            qs = q[:, hc * PEER_KEYS:(hc + 1) * PEER_KEYS].astype(BF16)
            s = lax.dot_general(keys_ref[hc], qs, (((1,), (1,)), ((), ())),
                                preferred_element_type=F32)
            vals, rank = _top16(s)
            planes.append((s, vals, rank))
        (s0, v0, rank0), (s1, v1, rank1) = planes
        v1s = jnp.concatenate(v1, axis=0)
        cand = jnp.concatenate([v0[a] + v1s for a in range(PEER_TOPK)], axis=0)
        _, crank = _top16(cand)
        chosen = crank < PEER_TOPK
        mx = v0[0] + v1[0]
        zsum = jnp.sum(jnp.where(chosen, jnp.exp(cand - mx), 0.0), axis=0, keepdims=True)
        cnt = jnp.zeros(s0.shape, F32)
        for a in range(PEER_TOPK):
            c_a = jnp.sum(jnp.where(chosen[a * PEER_TOPK:(a + 1) * PEER_TOPK], 1.0, 0.0),
                          axis=0, keepdims=True)
            cnt = jnp.where(rank0 == a, c_a, cnt)
        ac_ref[h, 0] = jnp.where(rank0 < PEER_TOPK, jnp.exp(s0 - v0[0]), 0.0) / zsum
        ac_ref[h, 1] = cnt
        br_ref[h, 0] = jnp.where(rank1 < PEER_TOPK, jnp.exp(s1 - v1[0]), 0.0).astype(BF16)
        br_ref[h, 1] = rank1.astype(F32).astype(BF16)


def peer_route(a2, w_pq, keys, tm=256):
    t, d = a2.shape
    tm = min(tm, t)
    plane = pl.BlockSpec((PEER_HEADS, 2, PEER_KEYS, tm), lambda i: (0, 0, 0, i))
    return pl.pallas_call(
        _route_kernel,
        grid=(t // tm,),
        in_specs=[
            pl.BlockSpec((tm, d), lambda i: (i, 0)),
            _resident(w_pq.shape),
            _resident(keys.shape),
        ],
        out_specs=[plane, plane],
        out_shape=[
            jax.ShapeDtypeStruct((PEER_HEADS, 2, PEER_KEYS, t), F32),
            jax.ShapeDtypeStruct((PEER_HEADS, 2, PEER_KEYS, t), BF16),
        ],
        compiler_params=_params(("parallel",)),
        name="peer_route",
    )(a2, w_pq, keys)


BF16_ROWS = 16


def _peer_kernel(at_ref, ac_ref, br_ref, u_ref, vt_ref, o_ref, acc_ref, *, ib, sub):
    ci = pl.program_id(1)
    tb = at_ref.shape[1]

    @pl.when(ci == 0)
    def _():
        acc_ref[...] = jnp.zeros(acc_ref.shape, F32)

    tiles = PEER_KEYS // BF16_ROWS
    for sc in range(ib // sub):
        es = slice(sc * sub * PEER_KEYS, (sc + 1) * sub * PEER_KEYS)
        hid = _gelu(jnp.dot(u_ref[es, :], at_ref[...], preferred_element_type=F32)).astype(BF16)
        parts = []
        for ii in range(sub):
            row = ci * ib + sc * sub + ii
            gate = None
            for h in range(PEER_HEADS):
                a_i = jnp.broadcast_to(ac_ref[h, 0, pl.ds(row, 1), :], (BF16_ROWS, tb)).astype(BF16)
                c_i = jnp.broadcast_to(ac_ref[h, 1, pl.ds(row, 1), :], (BF16_ROWS, tb)).astype(BF16)
                b = br_ref[h, 0].reshape(tiles, BF16_ROWS, tb)
                r = br_ref[h, 1].reshape(tiles, BF16_ROWS, tb)
                w = jnp.where(r < c_i[None], b * a_i[None], jnp.zeros((), BF16))
                gate = w if gate is None else gate + w
            parts.append(gate.reshape(PEER_KEYS, tb) * hid[ii * PEER_KEYS:(ii + 1) * PEER_KEYS, :])
        p = parts[0] if sub == 1 else jnp.concatenate(parts, axis=0)
        acc_ref[...] += jnp.dot(vt_ref[:, es], p, preferred_element_type=F32)

    @pl.when(ci == pl.num_programs(1) - 1)
    def _():
        o_ref[...] = acc_ref[...].T


def peer_experts(a2t, ac, br, u_bf, vt_bf, tb=512, ib=8, sub=8):
    d, t = a2t.shape
    tb = min(tb, t)
    ne = u_bf.shape[0]
    ec = ib * PEER_KEYS
    plane = pl.BlockSpec((PEER_HEADS, 2, PEER_KEYS, tb), lambda i, c: (0, 0, 0, i),
                         pipeline_mode=pl.Buffered(1))
    return pl.pallas_call(
        functools.partial(_peer_kernel, ib=ib, sub=sub),
        grid=(t // tb, ne // ec),
        in_specs=[
            pl.BlockSpec((d, tb), lambda i, c: (0, i), pipeline_mode=pl.Buffered(1)),
            plane,
            plane,
            pl.BlockSpec((ec, d), lambda i, c: (c, 0)),
            pl.BlockSpec((d, ec), lambda i, c: (0, c)),
        ],
        out_specs=pl.BlockSpec((tb, d), lambda i, c: (i, 0)),
        out_shape=jax.ShapeDtypeStruct((t, d), F32),
        scratch_shapes=[pltpu.VMEM((d, tb), F32)],
        compiler_params=_params(("parallel", "arbitrary")),
        name="peer_experts",
    )(a2t, ac, br, u_bf, vt_bf)


def _ple_kernel(h_ref, po_ref, p_ref, gp_ref, wg_ref, bg_ref, we_ref, gf_ref, y_ref):
    h = h_ref[...] + po_ref[...]
    a = _rms(h, gp_ref[...]).astype(BF16)
    gate = _sigmoid(jnp.dot(a, wg_ref[...], preferred_element_type=F32) + bg_ref[...])
    pe = jnp.dot(p_ref[...].astype(BF16), we_ref[...], preferred_element_type=F32)
    y_ref[...] = _rms(h + gate * pe, gf_ref[...])


def ple_final(h1, peer_out, p, g_ple, w_pg, b_pg, w_pe, g_final, tm=256):
    t, d = h1.shape
    tm = min(tm, t)
    vec = pl.BlockSpec((1, d), lambda i: (0, 0))
    return pl.pallas_call(
        _ple_kernel,
        grid=(t // tm,),
        in_specs=[
            pl.BlockSpec((tm, d), lambda i: (i, 0)),
            pl.BlockSpec((tm, d), lambda i: (i, 0)),
            pl.BlockSpec((tm, PLE_DIM), lambda i: (i, 0)),
            vec,
            _resident(w_pg.shape),
            vec,
            _resident(w_pe.shape),
            vec,
        ],
        out_specs=pl.BlockSpec((tm, d), lambda i: (i, 0)),
        out_shape=jax.ShapeDtypeStruct((t, d), F32),
        compiler_params=_params(("parallel",)),
        name="ple_final",
    )(h1, peer_out, p, g_ple, w_pg, b_pg, w_pe, g_final)


def _rope_tables(seq):
    rows = seq // GRID_W
    row = jnp.repeat(jnp.arange(rows, dtype=F32), GRID_W)
    col = jnp.tile(jnp.arange(GRID_W, dtype=F32), rows)
    n_axis_pairs = HEAD_DIM // 4
    freqs = ROPE_THETA ** (-jnp.arange(n_axis_pairs, dtype=F32) / n_axis_pairs)
    ang = jnp.concatenate([row[:, None] * freqs, col[:, None] * freqs], axis=-1)
    cos, sin = jnp.cos(ang), jnp.sin(ang)
    return jnp.concatenate([cos, cos], axis=-1), jnp.concatenate([-sin, sin], axis=-1)


def _head_perm():
    return np.concatenate([np.arange(0, HEAD_DIM, 2), np.arange(1, HEAD_DIM, 2)])


def _in_proj_columns():
    hp = _head_perm()
    q_cols = (np.arange(N_HEADS)[:, None] * HEAD_DIM + hp[None, :]).reshape(-1)
    k_cols = Q_W + (np.arange(N_KV_HEADS)[:, None] * HEAD_DIM + hp[None, :]).reshape(-1)
    v_cols = Q_W + KV_W + np.arange(KV_W)
    rest = Q_W + 2 * KV_W + np.arange(2 * SG_WIDTH + 2 * D_MODEL)
    return np.concatenate([q_cols, rest, k_cols, v_cols])


def _encode(x, p, wts):
    b, s, d = x.shape
    t = b * s
    x2 = x.reshape(t, d)
    cos, sin = _rope_tables(s)
    z = in_proj(x2, wts["g_mix"], wts["w_in"])
    q, k, v = qkv_prep(z, wts["g_q"], wts["g_k"], cos, sin, b, s)
    o = attention(q, k, v)
    sg = spatial_gating(z, wts["g_sg"], wts["w_s"], wts["sg_bias"])
    merged = branch_merge(o, sg, z, wts["w_br_attn"], wts["w_br_sg"])
    h1, a2, a2t = out_proj(merged, x2, wts["w_out"], wts["g_ffn"])
    ac, br = peer_route(a2, wts["w_pq"], wts["keys"])
    peer_out = peer_experts(a2t, ac, br, wts["u"], wts["vt"])
    y = ple_final(h1, peer_out, p.reshape(t, PLE_DIM), wts["g_ple"], wts["w_pg"], wts["b_pg"],
                  wts["w_pe"], wts["g_final"])
    return y.reshape(b, s, d)


def _prepare_weights(g_mix, w_in, g_q, g_k, g_sg, w_s, b_s, w_br_attn, w_br_sg, w_out, g_ffn,
                     w_pq, sub_keys, u_tab, v_tab, g_ple, w_pg, b_pg, w_pe, g_final):
    assert w_in.shape[0] == 1, "single-layer encoder"
    hp = _head_perm()
    gw = SG_WIDTH // SG_GROUPS
    return {
        "g_mix": g_mix[0][None, :],
        "w_in": w_in[0][:, _in_proj_columns()].astype(BF16),
        "g_q": g_q[0][hp][None, :],
        "g_k": g_k[0][hp][None, :],
        "g_sg": g_sg[0][None, :],
        "w_s": w_s[0].astype(BF16),
        "sg_bias": jnp.repeat(b_s[0].T, gw, axis=1),
        "w_br_attn": w_br_attn[0].astype(BF16),
        "w_br_sg": w_br_sg[0].astype(BF16),
        "w_out": w_out[0].astype(BF16),
        "g_ffn": g_ffn[0][None, :],
        "w_pq": w_pq[0].astype(BF16),
        "keys": sub_keys[0].reshape(2 * PEER_HEADS, PEER_KEYS, -1).astype(BF16),
        "u": u_tab[0].astype(BF16),
        "vt": v_tab[0].T.astype(BF16),
        "g_ple": g_ple[0][None, :],
        "w_pg": w_pg[0].astype(BF16),
        "b_pg": b_pg[0][None, :],
        "w_pe": w_pe[0].astype(BF16),
        "g_final": g_final[None, :],
    }


def kernel(x_prompt, x_sample, p_prompt, p_sample, g_mix, w_in, g_q, g_k, g_sg, w_s, b_s,
           w_br_attn, w_br_sg, w_out, g_ffn, w_pq, sub_keys, u_tab, v_tab, g_ple, w_pg,
           b_pg, w_pe, g_final):
    wts = _prepare_weights(g_mix, w_in, g_q, g_k, g_sg, w_s, b_s, w_br_attn, w_br_sg, w_out,
                           g_ffn, w_pq, sub_keys, u_tab, v_tab, g_ple, w_pg, b_pg, w_pe, g_final)
    return (_encode(x_prompt, p_prompt[0], wts), _encode(x_sample, p_sample[0], wts))
```
